```python
import math
import jax, jax.numpy as jnp
from jax import lax
import numpy as np

D_MODEL = 1024
BATCH = 4
SEQ = 8192
DEPTH = 2

N_MIXERS = 2
N_RET = (DEPTH + 1) // 2
N_LRU = DEPTH // 2

RET_HEADS = D_MODEL // 256
RET_DK = 256
RET_DV = 2 * RET_DK
RET_QK_W = RET_HEADS * RET_DK
RET_V_W = RET_HEADS * RET_DV
RET_CHUNK = 128
ROPE_BASE = 10000.0

LRU_WIDTH = 1536
LRU_BLOCK = 256
LRU_BLOCKS = LRU_WIDTH // LRU_BLOCK
LRU_CONV = 4
LRU_C = 8.0

D_FF = ((8 * D_MODEL // 3 + 255) // 256) * 256

NORM_EPS = 1e-6

kernel_name = "hybrid_retention_rglru_sandwich"


def rms_norm(x, g):
    xf = x.astype(jnp.float32)
    y = xf * lax.rsqrt(jnp.mean(xf * xf, axis=-1, keepdims=True) + NORM_EPS)
    return (y * g.astype(jnp.float32)).astype(x.dtype)


def rope(x, pos):
    half = x.shape[-1] // 2
    inv = 1.0 / (ROPE_BASE ** (jnp.arange(half, dtype=jnp.float32) / half))
    ang = pos.astype(jnp.float32)[:, None] * inv[None, :]
    cos = jnp.cos(ang)[None, :, None, :]
    sin = jnp.sin(ang)[None, :, None, :]
    x1, x2 = x[..., :half], x[..., half:]
    return jnp.concatenate([x1 * cos - x2 * sin, x1 * sin + x2 * cos], axis=-1)


def retention_mixer(x, w_in, w_out):
    B, S, _ = x.shape
    H, DK, DV, C = RET_HEADS, RET_DK, RET_DV, RET_CHUNK
    N = S // C
    proj = x @ w_in
    q, k, v, g = jnp.split(proj, [RET_QK_W, 2 * RET_QK_W, 2 * RET_QK_W + RET_V_W], axis=-1)
    pos = jnp.arange(S)
    q = rope(q.reshape(B, S, H, DK).astype(jnp.float32), pos)
    k = rope(k.reshape(B, S, H, DK).astype(jnp.float32), pos) * (DK ** -0.5)
    v = v.reshape(B, S, H, DV).astype(jnp.float32)

    log_gamma = jnp.log1p(-jnp.exp2(-5.0 - jnp.arange(H, dtype=jnp.float32)))
    idx = jnp.arange(C, dtype=jnp.float32)
    rel = idx[:, None] - idx[None, :]
    causal = rel >= 0
    decay = jnp.where(causal[None],
                      jnp.exp(jnp.where(causal, rel, 0.0)[None] * log_gamma[:, None, None]),
                      0.0)
    q_decay = jnp.exp((idx[:, None] + 1.0) * log_gamma[None, :])
    k_decay = jnp.exp((C - 1.0 - idx[:, None]) * log_gamma[None, :])
    chunk_decay = jnp.exp(C * log_gamma)

    qc = q.reshape(B, N, C, H, DK)
    kc = k.reshape(B, N, C, H, DK)
    vc = v.reshape(B, N, C, H, DV)

    scores = jnp.einsum('bnqhd,bnkhd->bnhqk', qc, kc) * decay[None, None]
    intra = jnp.einsum('bnhqk,bnkhe->bnqhe', scores, vc)

    def step(state, inp):
        qi, ki, vi = inp
        out = jnp.einsum('bchd,bhde->bche', qi, state) * q_decay[None, :, :, None]
        state = state * chunk_decay[None, :, None, None] + jnp.einsum(
            'bchd,bche->bhde', ki * k_decay[None, :, :, None], vi)
        return state, out

    state0 = jnp.zeros((B, H, DK, DV), jnp.float32)
    _, inter = lax.scan(step, state0, (qc.transpose(1, 0, 2, 3, 4),
                                       kc.transpose(1, 0, 2, 3, 4),
                                       vc.transpose(1, 0, 2, 3, 4)))
    o = (intra + inter.transpose(1, 0, 2, 3, 4)).reshape(B, S, H, DV)

    mu = jnp.mean(o, axis=-1, keepdims=True)
    oc = o - mu
    o = oc * lax.rsqrt(jnp.mean(oc * oc, axis=-1, keepdims=True) + NORM_EPS)
    o = o.reshape(B, S, RET_V_W).astype(x.dtype) * jax.nn.silu(g)
    return o @ w_out


def rglru_mixer(x, w_in, conv_w, conv_b, gate_w, gate_b, a_param, w_out):
    B, S, _ = x.shape
    proj = x @ w_in
    y_branch, u = jnp.split(proj, 2, axis=-1)
    y_branch = jax.nn.gelu(y_branch, approximate=True)

    u = lax.conv_general_dilated(
        u, conv_w[:, None, :], window_strides=(1,), padding=[(LRU_CONV - 1, 0)],
        dimension_numbers=('NWC', 'WIO', 'NWC'), feature_group_count=LRU_WIDTH) + conv_b

    ub = u.reshape(B, S, LRU_BLOCKS, LRU_BLOCK)
    gates = jnp.einsum('bsnk,gnkj->gbsnj', ub, gate_w) + gate_b[:, None, None]
    gates = jax.nn.sigmoid(gates.astype(jnp.float32)).reshape(2, B, S, LRU_WIDTH)
    r, i = gates[0], gates[1]

    log_a = -LRU_C * r * jax.nn.softplus(-a_param.astype(jnp.float32))
    a = jnp.exp(log_a)
    mult = jnp.sqrt(-jnp.expm1(2.0 * log_a))
    b = mult * (i * u.astype(jnp.float32))

    def combine(lhs, rhs):
        a1, b1 = lhs
        a2, b2 = rhs
        return a1 * a2, a2 * b1 + b2

    _, h = lax.associative_scan(combine, (a, b), axis=1)
    return (h.astype(x.dtype) * y_branch) @ w_out


def swiglu_ffn(x, w_in, w_out):
    gate, up = jnp.split(x @ w_in, 2, axis=-1)
    return (jax.nn.silu(gate) * up) @ w_out


def setup_inputs(seed: int = 0) -> dict:
    key = jax.random.key(seed)
    ks = jax.random.split(key, 16)
    f32 = jnp.float32

    def normal(k, shape, fan_in):
        return jax.random.normal(k, shape, f32) * (fan_in ** -0.5)

    x = jax.random.normal(ks[0], (BATCH, SEQ, D_MODEL), f32)
    ret_w_in = normal(ks[1], (N_RET, D_MODEL, 2 * RET_QK_W + 2 * RET_V_W), D_MODEL)
    ret_w_out = normal(ks[2], (N_RET, RET_V_W, D_MODEL), RET_V_W)
    lru_w_in = normal(ks[3], (N_LRU, D_MODEL, 2 * LRU_WIDTH), D_MODEL)
    lru_conv_w = normal(ks[4], (N_LRU, LRU_CONV, LRU_WIDTH), LRU_CONV)
    lru_conv_b = 0.01 * jax.random.normal(ks[5], (N_LRU, LRU_WIDTH), f32)
    lru_gate_w = normal(ks[6], (N_LRU, 2, LRU_BLOCKS, LRU_BLOCK, LRU_BLOCK), LRU_BLOCK)
    lru_gate_b = 0.01 * jax.random.normal(ks[7], (N_LRU, 2, LRU_BLOCKS, LRU_BLOCK), f32)
    a0 = jax.random.uniform(ks[8], (N_LRU, LRU_WIDTH), f32, minval=0.9, maxval=0.999)
    lru_a_param = jnp.log(a0) - jnp.log1p(-a0)
    lru_w_out = normal(ks[9], (N_LRU, LRU_WIDTH, D_MODEL), LRU_WIDTH)
    norm_g = 1.0 + 0.02 * jax.random.normal(ks[10], (DEPTH, 4, D_MODEL), f32)
    ffn_w_in = normal(ks[11], (DEPTH, D_MODEL, 2 * D_FF), D_MODEL)
    ffn_w_out = normal(ks[12], (DEPTH, D_FF, D_MODEL), D_FF)
    return {"x": x, "ret_w_in": ret_w_in, "ret_w_out": ret_w_out,
            "lru_w_in": lru_w_in, "lru_conv_w": lru_conv_w, "lru_conv_b": lru_conv_b,
            "lru_gate_w": lru_gate_w, "lru_gate_b": lru_gate_b, "lru_a_param": lru_a_param,
            "lru_w_out": lru_w_out, "norm_g": norm_g,
            "ffn_w_in": ffn_w_in, "ffn_w_out": ffn_w_out}


def reference(x, ret_w_in, ret_w_out, lru_w_in, lru_conv_w, lru_conv_b, lru_gate_w,
              lru_gate_b, lru_a_param, lru_w_out, norm_g, ffn_w_in, ffn_w_out):
    for layer in range(DEPTH):
        j = layer // N_MIXERS
        h = rms_norm(x, norm_g[layer, 0])
        if layer % N_MIXERS == 0:
            m = retention_mixer(h, ret_w_in[j], ret_w_out[j])
        else:
            m = rglru_mixer(h, lru_w_in[j], lru_conv_w[j], lru_conv_b[j], lru_gate_w[j],
                            lru_gate_b[j], lru_a_param[j], lru_w_out[j])
        x = x + rms_norm(m, norm_g[layer, 1])
        f = swiglu_ffn(rms_norm(x, norm_g[layer, 2]), ffn_w_in[layer], ffn_w_out[layer])
        x = x + rms_norm(f, norm_g[layer, 3])
    return x
```

```python
import functools
import math

import jax
import jax.numpy as jnp
from jax import lax
from jax.experimental import pallas as pl
from jax.experimental.pallas import tpu as pltpu

F32 = jnp.float32
BF16 = jnp.bfloat16

NORM_EPS = 1e-6
ROPE_BASE = 10000.0

RET_DK = 256
RET_DV = 512
RET_CHUNK = 256
LRU_BLOCK = 256
LRU_CONV = 4
LRU_C = 8.0
SUBLANES = 8
FFN_CHUNK = 256

MIX_ROWS = 256
FFN_ROWS = 512
VMEM_LIMIT = 56 * 1024 * 1024


def _rms(x, g):
    ms = jnp.mean(x * x, axis=-1, keepdims=True)
    return x * lax.rsqrt(ms + NORM_EPS) * g


def _dot(a, b):
    return jnp.dot(a, b, preferred_element_type=F32)


def _dot_nt(a, b):
    return lax.dot_general(a, b, (((1,), (1,)), ((), ())), preferred_element_type=F32)


def _dot_tn(a, b):
    return lax.dot_general(a, b, (((0,), (0,)), ((), ())), preferred_element_type=F32)


def _const_spec(shape):
    zeros = (0,) * len(shape)
    return pl.BlockSpec(shape, lambda *_: zeros, pipeline_mode=pl.Buffered(1))


def _rope(t, cos, sin):
    half = t.shape[-1] // 2
    t1, t2 = t[:, :half], t[:, half:]
    return jnp.concatenate([t1 * cos - t2 * sin, t1 * sin + t2 * cos], axis=-1)


def _ret_kernel(x_ref, g_ref, cos_ref, sin_ref, win_ref, wout_ref, o_ref,
                state_ref, decay_ref, qdec_ref, kdec_ref, *, heads):
    c = RET_CHUNK
    qk_w = heads * RET_DK
    v_w = heads * RET_DV
    log_gamma = [math.log1p(-2.0 ** (-5.0 - h)) for h in range(heads)]

    @pl.when((pl.program_id(0) == 0) & (pl.program_id(1) == 0))
    def _init_tables():
        row = lax.broadcasted_iota(jnp.int32, (c, c), 0)
        col = lax.broadcasted_iota(jnp.int32, (c, c), 1)
        rel = (row - col).astype(F32)
        causal = row >= col
        rq = lax.broadcasted_iota(jnp.int32, (c, RET_DK), 0).astype(F32)
        for h in range(heads):
            lg = log_gamma[h]
            scale = RET_DK ** -0.5
            decay_ref[h] = jnp.where(causal, jnp.exp(jnp.where(causal, rel, 0.0) * lg), 0.0) * scale
            qdec_ref[h] = jnp.exp((rq + 1.0) * lg)
            kdec_ref[h] = jnp.exp((c - 1.0 - rq) * lg) * scale

    @pl.when(pl.program_id(1) == 0)
    def _reset_state():
        state_ref[...] = jnp.zeros_like(state_ref)

    x = x_ref[0]
    hn = _rms(x, g_ref[0:1, :]).astype(BF16)
    cos = cos_ref[...]
    sin = sin_ref[...]

    m = None
    for h in range(heads):
        q = _dot(hn, win_ref[:, h * RET_DK:(h + 1) * RET_DK])
        k = _dot(hn, win_ref[:, qk_w + h * RET_DK:qk_w + (h + 1) * RET_DK])
        v = _dot(hn, win_ref[:, 2 * qk_w + h * RET_DV:2 * qk_w + (h + 1) * RET_DV])
        gate = _dot(hn, win_ref[:, 2 * qk_w + v_w + h * RET_DV:2 * qk_w + v_w + (h + 1) * RET_DV])
        q = _rope(q, cos, sin)
        k = _rope(k, cos, sin)
        qb = q.astype(BF16)
        kb = k.astype(BF16)
        vb = v.astype(BF16)
        qd = (q * qdec_ref[h]).astype(BF16)
        kd = (k * kdec_ref[h]).astype(BF16)

        state = state_ref[h]
        scores = (_dot_nt(qb, kb) * decay_ref[h]).astype(BF16)
        o = _dot(scores, vb) + _dot(qd, state.astype(BF16))
        state_ref[h] = state * math.exp(c * log_gamma[h]) + _dot_tn(kd, vb)

        mu = jnp.mean(o, axis=-1, keepdims=True)
        oc = o - mu
        on = oc * lax.rsqrt(jnp.mean(oc * oc, axis=-1, keepdims=True) + NORM_EPS)
        og = (on * (gate * jax.nn.sigmoid(gate))).astype(BF16)
        part = _dot(og, wout_ref[h * RET_DV:(h + 1) * RET_DV, :])
        m = part if m is None else m + part

    o_ref[0] = x + _rms(m, g_ref[1:2, :])


def _retention_layer(x, gains, w_in, w_out, cos, sin):
    b, s, d = x.shape
    heads = w_out.shape[0] // RET_DV
    ts = MIX_ROWS
    assert ts == RET_CHUNK and s % ts == 0
    kern = functools.partial(_ret_kernel, heads=heads)
    return pl.pallas_call(
        kern,
        grid=(b, s // ts),
        in_specs=[
            pl.BlockSpec((1, ts, d), lambda i, j: (i, j, 0)),
            _const_spec(gains.shape),
            pl.BlockSpec((ts, RET_DK // 2), lambda i, j: (j, 0)),
            pl.BlockSpec((ts, RET_DK // 2), lambda i, j: (j, 0)),
            _const_spec(w_in.shape),
            _const_spec(w_out.shape),
        ],
        out_specs=pl.BlockSpec((1, ts, d), lambda i, j: (i, j, 0)),
        out_shape=jax.ShapeDtypeStruct(x.shape, x.dtype),
        scratch_shapes=[
            pltpu.VMEM((heads, RET_DK, RET_DV), F32),
            pltpu.VMEM((heads, RET_CHUNK, RET_CHUNK), F32),
            pltpu.VMEM((heads, RET_CHUNK, RET_DK), F32),
            pltpu.VMEM((heads, RET_CHUNK, RET_DK), F32),
        ],
        compiler_params=pltpu.CompilerParams(
            dimension_semantics=("arbitrary", "arbitrary"),
            vmem_limit_bytes=VMEM_LIMIT),
        name="retention_mixer",
    )(x, gains, cos, sin, w_in, w_out)


def _lru_kernel(x_ref, g_ref, win_ref, convw_ref, convb_ref, gw_ref, gb_ref,
                ap_ref, wout_ref, o_ref, ubuf_ref, a_ref, b_ref, carry_ref, *, width):
    ts = x_ref.shape[1]
    nblk = width // LRU_BLOCK
    pad = SUBLANES

    @pl.when(pl.program_id(1) == 0)
    def _reset():
        ubuf_ref[0:pad, :] = jnp.zeros((pad, width), F32)
        carry_ref[...] = jnp.zeros_like(carry_ref)

    x = x_ref[0]
    hn = _rms(x, g_ref[0:1, :]).astype(BF16)

    ubuf_ref[pad:pad + ts, :] = _dot(hn, win_ref[:, width:2 * width])

    for n in range(nblk):
        cols = slice(n * LRU_BLOCK, (n + 1) * LRU_BLOCK)
        u = convb_ref[:, cols]
        for j in range(LRU_CONV):
            off = pad - (LRU_CONV - 1) + j
            u = u + convw_ref[j:j + 1, cols] * ubuf_ref[off:off + ts, cols]
        ub = u.astype(BF16)
        r = jax.nn.sigmoid(_dot(ub, gw_ref[0, n]) + gb_ref[0:1, cols])
        i = jax.nn.sigmoid(_dot(ub, gw_ref[1, n]) + gb_ref[1:2, cols])
        log_a = (-LRU_C) * r * jax.nn.softplus(-ap_ref[:, cols])
        a = jnp.exp(log_a)
        a_ref[:, cols] = a
        b_ref[:, cols] = jnp.sqrt(-jnp.tanh(log_a) * (1.0 + a * a)) * (i * u)

    ubuf_ref[0:pad, :] = ubuf_ref[ts:ts + pad, :]

    sub = lax.broadcasted_iota(jnp.int32, (SUBLANES, width), 0)

    def tile(t, carry):
        r0 = pl.multiple_of(t * SUBLANES, SUBLANES)
        a = a_ref[pl.ds(r0, SUBLANES), :]
        bb = b_ref[pl.ds(r0, SUBLANES), :]
        d = 1
        while d < SUBLANES:
            keep = sub >= d
            a_prev = pltpu.roll(a, d, axis=0)
            b_prev = pltpu.roll(bb, d, axis=0)
            bb = jnp.where(keep, a * b_prev + bb, bb)
            a = jnp.where(keep, a * a_prev, a)
            d *= 2
        hh = a * carry + bb
        b_ref[pl.ds(r0, SUBLANES), :] = hh
        return hh[SUBLANES - 1:SUBLANES, :]

    carry_ref[...] = lax.fori_loop(0, ts // SUBLANES, tile, carry_ref[...], unroll=2)

    y = jax.nn.gelu(_dot(hn, win_ref[:, 0:width]), approximate=True)
    m = _dot((b_ref[...] * y).astype(BF16), wout_ref[...])
    o_ref[0] = x + _rms(m, g_ref[1:2, :])


def _lru_layer(x, gains, w_in, conv_w, conv_b, gate_w, gate_b, a_param, w_out):
    b, s, d = x.shape
    width = w_out.shape[0]
    ts = MIX_ROWS
    assert s % ts == 0 and width % LRU_BLOCK == 0
    kern = functools.partial(_lru_kernel, width=width)
    return pl.pallas_call(
        kern,
        grid=(b, s // ts),
        in_specs=[
            pl.BlockSpec((1, ts, d), lambda i, j: (i, j, 0)),
            _const_spec(gains.shape),
            _const_spec(w_in.shape),
            _const_spec(conv_w.shape),
            _const_spec(conv_b.shape),
            _const_spec(gate_w.shape),
            _const_spec(gate_b.shape),
            _const_spec(a_param.shape),
            _const_spec(w_out.shape),
        ],
        out_specs=pl.BlockSpec((1, ts, d), lambda i, j: (i, j, 0)),
        out_shape=jax.ShapeDtypeStruct(x.shape, x.dtype),
        scratch_shapes=[
            pltpu.VMEM((ts + SUBLANES, width), F32),
            pltpu.VMEM((ts, width), F32),
            pltpu.VMEM((ts, width), F32),
            pltpu.VMEM((1, width), F32),
        ],
        compiler_params=pltpu.CompilerParams(
            dimension_semantics=("arbitrary", "arbitrary"),
            vmem_limit_bytes=VMEM_LIMIT),
        name="rglru_mixer",
    )(x, gains, w_in, conv_w, conv_b, gate_w, gate_b, a_param, w_out)


def _ffn_kernel(x_ref, g_ref, win_ref, wout_ref, o_ref, *, d_ff):
    x = x_ref[...]
    hn = _rms(x, g_ref[0:1, :]).astype(BF16)
    f = None
    for n in range(d_ff // FFN_CHUNK):
        gate = _dot(hn, win_ref[:, n * FFN_CHUNK:(n + 1) * FFN_CHUNK])
        up = _dot(hn, win_ref[:, d_ff + n * FFN_CHUNK:d_ff + (n + 1) * FFN_CHUNK])
        act = (gate * jax.nn.sigmoid(gate) * up).astype(BF16)
        part = _dot(act, wout_ref[n * FFN_CHUNK:(n + 1) * FFN_CHUNK, :])
        f = part if f is None else f + part
    o_ref[...] = x + _rms(f, g_ref[1:2, :])


def _ffn_layer(x, gains, w_in, w_out):
    b, s, d = x.shape
    d_ff = w_out.shape[0]
    tm = FFN_ROWS
    t = b * s
    assert t % tm == 0 and d_ff % FFN_CHUNK == 0
    kern = functools.partial(_ffn_kernel, d_ff=d_ff)
    out = pl.pallas_call(
        kern,
        grid=(t // tm,),
        in_specs=[
            pl.BlockSpec((tm, d), lambda i: (i, 0)),
            _const_spec(gains.shape),
            _const_spec(w_in.shape),
            _const_spec(w_out.shape),
        ],
        out_specs=pl.BlockSpec((tm, d), lambda i: (i, 0)),
        out_shape=jax.ShapeDtypeStruct((t, d), x.dtype),
        compiler_params=pltpu.CompilerParams(
            dimension_semantics=("arbitrary",),
            vmem_limit_bytes=VMEM_LIMIT),
        name="swiglu_ffn",
    )(x.reshape(t, d), gains, w_in, w_out)
    return out.reshape(b, s, d)


def _rope_tables(seq):
    half = RET_DK // 2
    inv = 1.0 / (ROPE_BASE ** (jnp.arange(half, dtype=F32) / half))
    ang = jnp.arange(seq).astype(F32)[:, None] * inv[None, :]
    return jnp.cos(ang), jnp.sin(ang)


def kernel(x, ret_w_in, ret_w_out, lru_w_in, lru_conv_w, lru_conv_b, lru_gate_w, lru_gate_b, lru_a_param, lru_w_out, norm_g, ffn_w_in, ffn_w_out):
    depth = norm_g.shape[0]
    cos, sin = _rope_tables(x.shape[1])
    for layer in range(depth):
        j = layer // 2
        if layer % 2 == 0:
            x = _retention_layer(x, norm_g[layer, 0:2], ret_w_in[j].astype(BF16),
                                 ret_w_out[j].astype(BF16), cos, sin)
        else:
            x = _lru_layer(x, norm_g[layer, 0:2], lru_w_in[j].astype(BF16), lru_conv_w[j],
                           lru_conv_b[j][None, :], lru_gate_w[j].astype(BF16), lru_gate_b[j].reshape(2, -1),
                           lru_a_param[j][None, :], lru_w_out[j].astype(BF16))
        x = _ffn_layer(x, norm_g[layer, 2:4], ffn_w_in[layer].astype(BF16), ffn_w_out[layer].astype(BF16))
    return x
```

```python
import functools
import math

import jax
import jax.numpy as jnp
from jax import lax
from jax.experimental import pallas as pl
from jax.experimental.pallas import tpu as pltpu

F32 = jnp.float32
BF16 = jnp.bfloat16

NORM_EPS = 1e-6
ROPE_BASE = 10000.0
F32_TINY = float(jnp.finfo(jnp.float32).tiny)

RET_DK = 256
RET_DV = 512
RET_CHUNK = 256
LRU_BLOCK = 256
LRU_CONV = 4
LRU_C = 8.0
SUBLANES = 8
FFN_CHUNK = 256

MIX_ROWS = 256
FFN_ROWS = 512
VMEM_LIMIT = 56 * 1024 * 1024


def _rms(x, g):
    ms = jnp.mean(x * x, axis=-1, keepdims=True)
    return x * lax.rsqrt(ms + NORM_EPS) * g


def _dot(a, b):
    return jnp.dot(a, b, preferred_element_type=F32)


def _dot_nt(a, b):
    return lax.dot_general(a, b, (((1,), (1,)), ((), ())), preferred_element_type=F32)


def _const_spec(shape):
    zeros = (0,) * len(shape)
    return pl.BlockSpec(shape, lambda *_: zeros, pipeline_mode=pl.Buffered(1))


def _ret_kernel(x_ref, g_ref, cos_ref, sin_ref, cost_ref, sint_ref, win_ref, wkt_ref, wout_ref,
                o_ref, state_ref, decay_ref, qdec_ref, kdec_ref, og_ref, *, heads):
    c = RET_CHUNK
    half = RET_DK // 2
    qk_w = heads * RET_DK
    v_w = heads * RET_DV
    log_gamma = [math.log1p(-2.0 ** (-5.0 - h)) for h in range(heads)]

    @pl.when((pl.program_id(0) == 0) & (pl.program_id(1) == 0))
    def _init_tables():
        row = lax.broadcasted_iota(jnp.int32, (c, c), 0)
        col = lax.broadcasted_iota(jnp.int32, (c, c), 1)
        rel = (row - col).astype(F32)
        causal = row >= col
        rq = lax.broadcasted_iota(jnp.int32, (c, RET_DK), 0).astype(F32)
        ck = lax.broadcasted_iota(jnp.int32, (RET_DK, c), 1).astype(F32)
        for h in range(heads):
            lg = log_gamma[h]
            scale = RET_DK ** -0.5
            decay_ref[h] = jnp.where(causal, jnp.exp(jnp.where(causal, rel, 0.0) * lg), 0.0) * scale
            qdec_ref[h] = jnp.exp((rq + 1.0) * lg)
            kdec_ref[h] = jnp.exp((c - 1.0 - ck) * lg) * scale

    @pl.when(pl.program_id(1) == 0)
    def _reset_state():
        state_ref[...] = jnp.zeros_like(state_ref)

    x = x_ref[0]
    hn = _rms(x, g_ref[0:1, :]).astype(BF16)
    cos = cos_ref[...]
    sin = sin_ref[...]
    cos_t = cost_ref[...]
    sin_t = sint_ref[...]

    k_all = _dot_nt(wkt_ref[...], hn)

    for h in range(heads):
        q = _dot(hn, win_ref[:, h * RET_DK:(h + 1) * RET_DK])
        v = _dot(hn, win_ref[:, 2 * qk_w + h * RET_DV:2 * qk_w + (h + 1) * RET_DV])
        gate = _dot(hn, win_ref[:, 2 * qk_w + v_w + h * RET_DV:2 * qk_w + v_w + (h + 1) * RET_DV])
        q1, q2 = q[:, :half], q[:, half:]
        q = jnp.concatenate([q1 * cos - q2 * sin, q1 * sin + q2 * cos], axis=-1)
        k1 = k_all[h * RET_DK:h * RET_DK + half, :]
        k2 = k_all[h * RET_DK + half:(h + 1) * RET_DK, :]
        k = jnp.concatenate([k1 * cos_t - k2 * sin_t, k1 * sin_t + k2 * cos_t], axis=0)
        qb = q.astype(BF16)
        kb = k.astype(BF16)
        vb = v.astype(BF16)
        qd = (q * qdec_ref[h]).astype(BF16)
        kd = (k * kdec_ref[h]).astype(BF16)

        state = state_ref[h]
        scores = (_dot(qb, kb) * decay_ref[h]).astype(BF16)
        o = _dot(scores, vb) + _dot(qd, state.astype(BF16))
        state_ref[h] = state * math.exp(c * log_gamma[h]) + _dot(kd, vb)

        mu = jnp.mean(o, axis=-1, keepdims=True)
        oc = o - mu
        on = oc * lax.rsqrt(jnp.mean(oc * oc, axis=-1, keepdims=True) + NORM_EPS)
        og_ref[:, h * RET_DV:(h + 1) * RET_DV] = (on * (gate * jax.nn.sigmoid(gate))).astype(BF16)

    m = _dot(og_ref[...], wout_ref[...])
    o_ref[0] = x + _rms(m, g_ref[1:2, :])


def _retention_layer(x, gains, w_in, w_kt, w_out, cos, sin):
    b, s, d = x.shape
    heads = w_out.shape[0] // RET_DV
    ts = MIX_ROWS
    half = RET_DK // 2
    assert ts == RET_CHUNK and s % ts == 0
    kern = functools.partial(_ret_kernel, heads=heads)
    return pl.pallas_call(
        kern,
        grid=(b, s // ts),
        in_specs=[
            pl.BlockSpec((1, ts, d), lambda i, j: (i, j, 0)),
            _const_spec(gains.shape),
            pl.BlockSpec((ts, half), lambda i, j: (j, 0)),
            pl.BlockSpec((ts, half), lambda i, j: (j, 0)),
            pl.BlockSpec((half, ts), lambda i, j: (0, j)),
            pl.BlockSpec((half, ts), lambda i, j: (0, j)),
            _const_spec(w_in.shape),
            _const_spec(w_kt.shape),
            _const_spec(w_out.shape),
        ],
        out_specs=pl.BlockSpec((1, ts, d), lambda i, j: (i, j, 0)),
        out_shape=jax.ShapeDtypeStruct(x.shape, x.dtype),
        scratch_shapes=[
            pltpu.VMEM((heads, RET_DK, RET_DV), F32),
            pltpu.VMEM((heads, RET_CHUNK, RET_CHUNK), F32),
            pltpu.VMEM((heads, RET_CHUNK, RET_DK), F32),
            pltpu.VMEM((heads, RET_DK, RET_CHUNK), F32),
            pltpu.VMEM((ts, heads * RET_DV), BF16),
        ],
        compiler_params=pltpu.CompilerParams(
            dimension_semantics=("arbitrary", "arbitrary"),
            vmem_limit_bytes=VMEM_LIMIT),
        name="retention_mixer",
    )(x, gains, cos, sin, cos.T, sin.T, w_in, w_kt, w_out)


def _sublane_scan(a, b, sub):
    d = 1
    while d < SUBLANES:
        keep = sub >= d
        a_prev = pltpu.roll(a, d, axis=0)
        b_prev = pltpu.roll(b, d, axis=0)
        b = jnp.where(keep, a * b_prev + b, b)
        a = jnp.where(keep, a * a_prev, a)
        d *= 2
    return a, b


def _lru_kernel(x_ref, g_ref, win_ref, convw_ref, convb_ref, gw_ref, gb_ref, ap_ref, wout_ref,
                o_ref, perm_ref, permt_ref, ubuf_ref, tail_ref, carry_ref, hy_ref, *, width):
    ts = x_ref.shape[1]
    seg = ts // SUBLANES
    nblk = width // LRU_BLOCK
    halo = (LRU_CONV - 1) * SUBLANES

    @pl.when((pl.program_id(0) == 0) & (pl.program_id(1) == 0))
    def _init_perm():
        row = lax.broadcasted_iota(jnp.int32, (ts, ts), 0)
        col = lax.broadcasted_iota(jnp.int32, (ts, ts), 1)
        perm_ref[...] = jnp.where(col == (row % SUBLANES) * seg + row // SUBLANES, 1.0, 0.0).astype(BF16)
        permt_ref[...] = jnp.where(row == (col % SUBLANES) * seg + col // SUBLANES, 1.0, 0.0).astype(BF16)

    @pl.when(pl.program_id(1) == 0)
    def _reset():
        tail_ref[...] = jnp.zeros_like(tail_ref)
        carry_ref[...] = jnp.zeros_like(carry_ref)

    x = x_ref[0]
    hn = _rms(x, g_ref[0:1, :]).astype(BF16)
    hp = _dot(perm_ref[...], hn).astype(BF16)

    u_all = _dot(hp, win_ref[:, width:2 * width])
    ubuf_ref[halo:halo + ts, :] = u_all
    sub_w = lax.broadcasted_iota(jnp.int32, (SUBLANES, width), 0)
    for k in range(1, LRU_CONV):
        cur = u_all[ts - k * SUBLANES:ts - (k - 1) * SUBLANES, :]
        prev = tail_ref[halo - k * SUBLANES:halo - (k - 1) * SUBLANES, :]
        ubuf_ref[halo - k * SUBLANES:halo - (k - 1) * SUBLANES, :] = jnp.where(
            sub_w == 0, pltpu.roll(prev, 1, axis=0), pltpu.roll(cur, 1, axis=0))
    tail_ref[...] = u_all[ts - halo:ts, :]

    sub = lax.broadcasted_iota(jnp.int32, (SUBLANES, LRU_BLOCK), 0)
    for n in range(nblk):
        cols = slice(n * LRU_BLOCK, (n + 1) * LRU_BLOCK)
        u = convb_ref[:, cols]
        for j in range(LRU_CONV):
            u = u + convw_ref[j:j + 1, cols] * ubuf_ref[j * SUBLANES:j * SUBLANES + ts, cols]
        ub = u.astype(BF16)
        r = jax.nn.sigmoid(_dot(ub, gw_ref[0, n]) + gb_ref[0:1, cols])
        i = jax.nn.sigmoid(_dot(ub, gw_ref[1, n]) + gb_ref[1:2, cols])
        log_a = (-LRU_C) * r * jax.nn.softplus(-ap_ref[:, cols])
        a = jnp.exp(log_a)
        z = -jnp.tanh(log_a) * (1.0 + a * a)
        b = z * lax.rsqrt(jnp.maximum(z, F32_TINY)) * (i * u)

        hs, ps = [], []
        h = p = None
        for l in range(seg):
            al = a[l * SUBLANES:(l + 1) * SUBLANES, :]
            bl = b[l * SUBLANES:(l + 1) * SUBLANES, :]
            h = bl if h is None else al * h + bl
            p = al if p is None else al * p
            hs.append(h)
            ps.append(p)
        carry = carry_ref[:, cols]
        pa, pb = _sublane_scan(p, h, sub)
        h_end = pa * carry + pb
        h_in = jnp.where(sub == 0, carry, pltpu.roll(h_end, 1, axis=0))
        carry_ref[:, cols] = jnp.broadcast_to(h_end[SUBLANES - 1:SUBLANES, :], (SUBLANES, LRU_BLOCK))

        y = jax.nn.gelu(_dot(hp, win_ref[:, cols]), approximate=True)
        hy = [(hs[l] + ps[l] * h_in) * y[l * SUBLANES:(l + 1) * SUBLANES, :] for l in range(seg)]
        hy_ref[:, cols] = jnp.concatenate(hy, axis=0).astype(BF16)

    hy_nat = _dot(permt_ref[...], hy_ref[...]).astype(BF16)
    m = _dot(hy_nat, wout_ref[...])
    o_ref[0] = x + _rms(m, g_ref[1:2, :])


def _lru_layer(x, gains, w_in, conv_w, conv_b, gate_w, gate_b, a_param, w_out):
    b, s, d = x.shape
    width = w_out.shape[0]
    ts = MIX_ROWS
    halo = (LRU_CONV - 1) * SUBLANES
    assert s % ts == 0 and width % LRU_BLOCK == 0 and ts // SUBLANES >= LRU_CONV
    kern = functools.partial(_lru_kernel, width=width)
    return pl.pallas_call(
        kern,
        grid=(b, s // ts),
        in_specs=[
            pl.BlockSpec((1, ts, d), lambda i, j: (i, j, 0)),
            _const_spec(gains.shape),
            _const_spec(w_in.shape),
            _const_spec(conv_w.shape),
            _const_spec(conv_b.shape),
            _const_spec(gate_w.shape),
            _const_spec(gate_b.shape),
            _const_spec(a_param.shape),
            _const_spec(w_out.shape),
        ],
        out_specs=pl.BlockSpec((1, ts, d), lambda i, j: (i, j, 0)),
        out_shape=jax.ShapeDtypeStruct(x.shape, x.dtype),
        scratch_shapes=[
            pltpu.VMEM((ts, ts), BF16),
            pltpu.VMEM((ts, ts), BF16),
            pltpu.VMEM((halo + ts, width), F32),
            pltpu.VMEM((halo, width), F32),
            pltpu.VMEM((SUBLANES, width), F32),
            pltpu.VMEM((ts, width), BF16),
        ],
        compiler_params=pltpu.CompilerParams(
            dimension_semantics=("arbitrary", "arbitrary"),
            vmem_limit_bytes=VMEM_LIMIT),
        name="rglru_mixer",
    )(x, gains, w_in, conv_w, conv_b, gate_w, gate_b, a_param, w_out)


def _ffn_kernel(x_ref, g_ref, win_ref, wout_ref, o_ref, *, d_ff):
    x = x_ref[...]
    hn = _rms(x, g_ref[0:1, :]).astype(BF16)
    f = None
    for n in range(d_ff // FFN_CHUNK):
        gate = _dot(hn, win_ref[:, n * FFN_CHUNK:(n + 1) * FFN_CHUNK])
        up = _dot(hn, win_ref[:, d_ff + n * FFN_CHUNK:d_ff + (n + 1) * FFN_CHUNK])
        act = (gate * jax.nn.sigmoid(gate) * up).astype(BF16)
        part = _dot(act, wout_ref[n * FFN_CHUNK:(n + 1) * FFN_CHUNK, :])
        f = part if f is None else f + part
    o_ref[...] = x + _rms(f, g_ref[1:2, :])


def _ffn_layer(x, gains, w_in, w_out):
    b, s, d = x.shape
    d_ff = w_out.shape[0]
    tm = FFN_ROWS
    t = b * s
    assert t % tm == 0 and d_ff % FFN_CHUNK == 0
    kern = functools.partial(_ffn_kernel, d_ff=d_ff)
    out = pl.pallas_call(
        kern,
        grid=(t // tm,),
        in_specs=[
            pl.BlockSpec((tm, d), lambda i: (i, 0)),
            _const_spec(gains.shape),
            _const_spec(w_in.shape),
            _const_spec(w_out.shape),
        ],
        out_specs=pl.BlockSpec((tm, d), lambda i: (i, 0)),
        out_shape=jax.ShapeDtypeStruct((t, d), x.dtype),
        compiler_params=pltpu.CompilerParams(
            dimension_semantics=("arbitrary",),
            vmem_limit_bytes=VMEM_LIMIT),
        name="swiglu_ffn",
    )(x.reshape(t, d), gains, w_in, w_out)
    return out.reshape(b, s, d)


def _rope_tables(seq):
    half = RET_DK // 2
    inv = 1.0 / (ROPE_BASE ** (jnp.arange(half, dtype=F32) / half))
    ang = jnp.arange(seq).astype(F32)[:, None] * inv[None, :]
    return jnp.cos(ang), jnp.sin(ang)


def kernel(x, ret_w_in, ret_w_out, lru_w_in, lru_conv_w, lru_conv_b, lru_gate_w, lru_gate_b, lru_a_param, lru_w_out, norm_g, ffn_w_in, ffn_w_out):
    depth = norm_g.shape[0]
    cos, sin = _rope_tables(x.shape[1])
    for layer in range(depth):
        j = layer // 2
        if layer % 2 == 0:
            heads = ret_w_out.shape[1] // RET_DV
            w_in = ret_w_in[j].astype(BF16)
            w_kt = w_in[:, heads * RET_DK:2 * heads * RET_DK].T
            x = _retention_layer(x, norm_g[layer, 0:2], w_in, w_kt, ret_w_out[j].astype(BF16), cos, sin)
        else:
            x = _lru_layer(x, norm_g[layer, 0:2], lru_w_in[j].astype(BF16), lru_conv_w[j],
                           lru_conv_b[j][None, :], lru_gate_w[j].astype(BF16), lru_gate_b[j].reshape(2, -1),
                           lru_a_param[j][None, :], lru_w_out[j].astype(BF16))
        x = _ffn_layer(x, norm_g[layer, 2:4], ffn_w_in[layer].astype(BF16), ffn_w_out[layer].astype(BF16))
    return x
```

```python
import functools
import math

import jax
import jax.numpy as jnp
import numpy as np
from jax import lax
from jax.experimental import pallas as pl
from jax.experimental.pallas import tpu as pltpu

F32 = jnp.float32
BF16 = jnp.bfloat16

NORM_EPS = 1e-6
ROPE_BASE = 10000.0
F32_TINY = float(jnp.finfo(jnp.float32).tiny)

RET_DK = 256
RET_DV = 512
RET_CHUNK = 256
LRU_BLOCK = 256
LRU_CONV = 4
LRU_C = 8.0
SUBLANES = 8
LANES = 128
FFN_CHUNK = 256

MIX_ROWS = 256
FFN_ROWS = 512
VMEM_LIMIT = 56 * 1024 * 1024


def _rms(x, g):
    ms = jnp.mean(x * x, axis=-1, keepdims=True)
    return x * lax.rsqrt(ms + NORM_EPS) * g


def _dot(a, b):
    return jnp.dot(a, b, preferred_element_type=F32)


def _dot_nt(a, b):
    return lax.dot_general(a, b, (((1,), (1,)), ((), ())), preferred_element_type=F32)


def _const_spec(shape):
    zeros = (0,) * len(shape)
    return pl.BlockSpec(shape, lambda *_: zeros, pipeline_mode=pl.Buffered(1))


def _layer_spec(stacked, layer):
    index = (layer,) + (0,) * (stacked.ndim - 1)
    return pl.BlockSpec((None,) + stacked.shape[1:], lambda *_: index, pipeline_mode=pl.Buffered(1))


def _ret_kernel(x_ref, g_ref, rs_ref, rst_ref, rw_ref, rwt_ref, win_ref, wkt_ref, wout_ref,
                o_ref, state_ref, decay_ref, qdec_ref, kdec_ref, og_ref, *, heads):
    c = RET_CHUNK
    half = RET_DK // 2
    qk_w = heads * RET_DK
    v_w = heads * RET_DV
    log_gamma = [math.log1p(-2.0 ** (-5.0 - h)) for h in range(heads)]

    @pl.when((pl.program_id(0) == 0) & (pl.program_id(1) == 0))
    def _init_tables():
        row = lax.broadcasted_iota(jnp.int32, (c, c), 0)
        col = lax.broadcasted_iota(jnp.int32, (c, c), 1)
        rel = (row - col).astype(F32)
        causal = row >= col
        rq = lax.broadcasted_iota(jnp.int32, (c, RET_DK), 0).astype(F32)
        ck = lax.broadcasted_iota(jnp.int32, (RET_DK, c), 1).astype(F32)
        for h in range(heads):
            lg = log_gamma[h]
            scale = RET_DK ** -0.5
            decay_ref[h] = jnp.where(causal, jnp.exp(jnp.where(causal, rel, 0.0) * lg), 0.0) * scale
            qdec_ref[h] = jnp.exp((rq + 1.0) * lg)
            kdec_ref[h] = jnp.exp((c - 1.0 - ck) * lg) * scale

    @pl.when(pl.program_id(1) == 0)
    def _reset_state():
        state_ref[...] = jnp.zeros_like(state_ref)

    x = x_ref[0]
    hn = _rms(x, g_ref[0:1, :]).astype(BF16)
    ca, sa = rs_ref[0, 0:1, :], rs_ref[0, 1:2, :]
    cb, sb = rw_ref[0], rw_ref[1]
    cos = ca * cb - sa * sb
    sin = sa * cb + ca * sb
    cat = jnp.concatenate([rst_ref[0, 0]] * (c // rst_ref.shape[-1]), axis=1)
    sat = jnp.concatenate([rst_ref[0, 1]] * (c // rst_ref.shape[-1]), axis=1)
    cbt, sbt = rwt_ref[0], rwt_ref[1]
    cos_t = cat * cbt - sat * sbt
    sin_t = sat * cbt + cat * sbt

    k_all = _dot_nt(wkt_ref[...], hn)

    for h in range(heads):
        q = _dot(hn, win_ref[:, h * RET_DK:(h + 1) * RET_DK])
        v = _dot(hn, win_ref[:, 2 * qk_w + h * RET_DV:2 * qk_w + (h + 1) * RET_DV])
        gate = _dot(hn, win_ref[:, 2 * qk_w + v_w + h * RET_DV:2 * qk_w + v_w + (h + 1) * RET_DV])
        q1, q2 = q[:, :half], q[:, half:]
        q = jnp.concatenate([q1 * cos - q2 * sin, q1 * sin + q2 * cos], axis=-1)
        k1 = k_all[h * RET_DK:h * RET_DK + half, :]
        k2 = k_all[h * RET_DK + half:(h + 1) * RET_DK, :]
        k = jnp.concatenate([k1 * cos_t - k2 * sin_t, k1 * sin_t + k2 * cos_t], axis=0)
        qb = q.astype(BF16)
        kb = k.astype(BF16)
        vb = v.astype(BF16)
        qd = (q * qdec_ref[h]).astype(BF16)
        kd = (k * kdec_ref[h]).astype(BF16)

        state = state_ref[h]
        scores = (_dot(qb, kb) * decay_ref[h]).astype(BF16)
        o = _dot(scores, vb) + _dot(qd, state.astype(BF16))
        state_ref[h] = state * math.exp(c * log_gamma[h]) + _dot(kd, vb)

        mu = jnp.mean(o, axis=-1, keepdims=True)
        oc = o - mu
        on = oc * lax.rsqrt(jnp.mean(oc * oc, axis=-1, keepdims=True) + NORM_EPS)
        og_ref[:, h * RET_DV:(h + 1) * RET_DV] = (on * (gate * jax.nn.sigmoid(gate))).astype(BF16)

    m = _dot(og_ref[...], wout_ref[...])
    o_ref[0] = x + _rms(m, g_ref[1:2, :])


def _rope_tables(seq, ts):
    half = RET_DK // 2
    inv = 1.0 / (ROPE_BASE ** (np.arange(half, dtype=np.float64) / half))
    a_start = (np.arange(seq // ts, dtype=np.float64) * ts)[:, None] * inv[None, :]
    a_within = np.arange(ts, dtype=np.float64)[:, None] * inv[None, :]
    start = np.zeros((seq // ts, SUBLANES, half), np.float32)
    start[:, 0], start[:, 1] = np.cos(a_start), np.sin(a_start)
    start_t = np.stack([np.cos(a_start), np.sin(a_start)], axis=1).astype(np.float32)
    within = np.stack([np.cos(a_within), np.sin(a_within)]).astype(np.float32)
    start_t = jnp.broadcast_to(jnp.asarray(start_t)[..., None], start_t.shape + (LANES,))
    return jnp.asarray(start), start_t, jnp.asarray(within), jnp.asarray(within.transpose(0, 2, 1))


def _retention_layer(x, gains, w_in, w_kt, w_out, layer):
    b, s, d = x.shape
    heads = w_out.shape[1] // RET_DV
    ts = MIX_ROWS
    half = RET_DK // 2
    assert ts == RET_CHUNK and s % ts == 0 and ts % LANES == 0
    rope_start, rope_start_t, rope_within, rope_within_t = _rope_tables(s, ts)
    kern = functools.partial(_ret_kernel, heads=heads)
    return pl.pallas_call(
        kern,
        grid=(b, s // ts),
        in_specs=[
            pl.BlockSpec((1, ts, d), lambda i, j: (i, j, 0)),
            _const_spec(gains.shape),
            pl.BlockSpec((1, SUBLANES, half), lambda i, j: (j, 0, 0)),
            pl.BlockSpec((1, 2, half, LANES), lambda i, j: (j, 0, 0, 0)),
            _const_spec(rope_within.shape),
            _const_spec(rope_within_t.shape),
            _layer_spec(w_in, layer),
            _const_spec(w_kt.shape),
            _layer_spec(w_out, layer),
        ],
        out_specs=pl.BlockSpec((1, ts, d), lambda i, j: (i, j, 0)),
        out_shape=jax.ShapeDtypeStruct(x.shape, x.dtype),
        scratch_shapes=[
            pltpu.VMEM((heads, RET_DK, RET_DV), F32),
            pltpu.VMEM((heads, RET_CHUNK, RET_CHUNK), F32),
            pltpu.VMEM((heads, RET_CHUNK, RET_DK), F32),
            pltpu.VMEM((heads, RET_DK, RET_CHUNK), F32),
            pltpu.VMEM((ts, heads * RET_DV), BF16),
        ],
        compiler_params=pltpu.CompilerParams(
            dimension_semantics=("arbitrary", "arbitrary"),
            vmem_limit_bytes=VMEM_LIMIT),
        name="retention_mixer",
    )(x, gains, rope_start, rope_start_t, rope_within, rope_within_t, w_in, w_kt, w_out)


def _sublane_scan(a, b, sub):
    d = 1
    while d < SUBLANES:
        keep = sub >= d
        a_prev = pltpu.roll(a, d, axis=0)
        b_prev = pltpu.roll(b, d, axis=0)
        b = jnp.where(keep, a * b_prev + b, b)
        a = jnp.where(keep, a * a_prev, a)
        d *= 2
    return a, b


def _lru_kernel(x_ref, g_ref, win_ref, convw_ref, convb_ref, gw_ref, gb_ref, ap_ref, wout_ref,
                o_ref, perm_ref, permt_ref, ubuf_ref, tail_ref, carry_ref, hy_ref, *, width):
    ts = x_ref.shape[1]
    seg = ts // SUBLANES
    nblk = width // LRU_BLOCK
    halo = (LRU_CONV - 1) * SUBLANES

    @pl.when((pl.program_id(0) == 0) & (pl.program_id(1) == 0))
    def _init_perm():
        row = lax.broadcasted_iota(jnp.int32, (ts, ts), 0)
        col = lax.broadcasted_iota(jnp.int32, (ts, ts), 1)
        perm_ref[...] = jnp.where(col == (row % SUBLANES) * seg + row // SUBLANES, 1.0, 0.0).astype(BF16)
        permt_ref[...] = jnp.where(row == (col % SUBLANES) * seg + col // SUBLANES, 1.0, 0.0).astype(BF16)

    @pl.when(pl.program_id(1) == 0)
    def _reset():
        tail_ref[...] = jnp.zeros_like(tail_ref)
        carry_ref[...] = jnp.zeros_like(carry_ref)

    x = x_ref[0]
    hn = _rms(x, g_ref[0:1, :]).astype(BF16)
    hp = _dot(perm_ref[...], hn).astype(BF16)

    u_all = _dot(hp, win_ref[:, width:2 * width])
    ubuf_ref[halo:halo + ts, :] = u_all
    sub_w = lax.broadcasted_iota(jnp.int32, (SUBLANES, width), 0)
    for k in range(1, LRU_CONV):
        cur = u_all[ts - k * SUBLANES:ts - (k - 1) * SUBLANES, :]
        prev = tail_ref[halo - k * SUBLANES:halo - (k - 1) * SUBLANES, :]
        ubuf_ref[halo - k * SUBLANES:halo - (k - 1) * SUBLANES, :] = jnp.where(
            sub_w == 0, pltpu.roll(prev, 1, axis=0), pltpu.roll(cur, 1, axis=0))
    tail_ref[...] = u_all[ts - halo:ts, :]

    sub = lax.broadcasted_iota(jnp.int32, (SUBLANES, LRU_BLOCK), 0)
    for n in range(nblk):
        cols = slice(n * LRU_BLOCK, (n + 1) * LRU_BLOCK)
        u = convb_ref[:, cols]
        for j in range(LRU_CONV):
            u = u + convw_ref[j:j + 1, cols] * ubuf_ref[j * SUBLANES:j * SUBLANES + ts, cols]
        ub = u.astype(BF16)
        r = jax.nn.sigmoid(_dot(ub, gw_ref[0, n]) + gb_ref[0:1, cols])
        i = jax.nn.sigmoid(_dot(ub, gw_ref[1, n]) + gb_ref[1:2, cols])
        log_a = (-LRU_C) * r * jax.nn.softplus(-ap_ref[:, cols])
        a = jnp.exp(log_a)
        z = -jnp.tanh(log_a) * (1.0 + a * a)
        b = z * lax.rsqrt(jnp.maximum(z, F32_TINY)) * (i * u)

        hs, ps = [], []
        h = p = None
        for l in range(seg):
            al = a[l * SUBLANES:(l + 1) * SUBLANES, :]
            bl = b[l * SUBLANES:(l + 1) * SUBLANES, :]
            h = bl if h is None else al * h + bl
            p = al if p is None else al * p
            hs.append(h)
            ps.append(p)
        carry = carry_ref[:, cols]
        pa, pb = _sublane_scan(p, h, sub)
        h_end = pa * carry + pb
        h_in = jnp.where(sub == 0, carry, pltpu.roll(h_end, 1, axis=0))
        carry_ref[:, cols] = jnp.broadcast_to(h_end[SUBLANES - 1:SUBLANES, :], (SUBLANES, LRU_BLOCK))

        y = jax.nn.gelu(_dot(hp, win_ref[:, cols]), approximate=True)
        hy = [(hs[l] + ps[l] * h_in) * y[l * SUBLANES:(l + 1) * SUBLANES, :] for l in range(seg)]
        hy_ref[:, cols] = jnp.concatenate(hy, axis=0).astype(BF16)

    hy_nat = _dot(permt_ref[...], hy_ref[...]).astype(BF16)
    m = _dot(hy_nat, wout_ref[...])
    o_ref[0] = x + _rms(m, g_ref[1:2, :])


def _lru_layer(x, gains, w_in, conv_w, conv_b, gate_w, gate_b, a_param, w_out, layer):
    b, s, d = x.shape
    width = w_out.shape[1]
    ts = MIX_ROWS
    halo = (LRU_CONV - 1) * SUBLANES
    assert s % ts == 0 and width % LRU_BLOCK == 0 and ts // SUBLANES >= LRU_CONV
    kern = functools.partial(_lru_kernel, width=width)
    return pl.pallas_call(
        kern,
        grid=(b, s // ts),
        in_specs=[
            pl.BlockSpec((1, ts, d), lambda i, j: (i, j, 0)),
            _const_spec(gains.shape),
            _layer_spec(w_in, layer),
            _layer_spec(conv_w, layer),
            _layer_spec(conv_b, layer),
            _layer_spec(gate_w, layer),
            _layer_spec(gate_b, layer),
            _layer_spec(a_param, layer),
            _layer_spec(w_out, layer),
        ],
        out_specs=pl.BlockSpec((1, ts, d), lambda i, j: (i, j, 0)),
        out_shape=jax.ShapeDtypeStruct(x.shape, x.dtype),
        scratch_shapes=[
            pltpu.VMEM((ts, ts), BF16),
            pltpu.VMEM((ts, ts), BF16),
            pltpu.VMEM((halo + ts, width), F32),
            pltpu.VMEM((halo, width), F32),
            pltpu.VMEM((SUBLANES, width), F32),
            pltpu.VMEM((ts, width), BF16),
        ],
        compiler_params=pltpu.CompilerParams(
            dimension_semantics=("arbitrary", "arbitrary"),
            vmem_limit_bytes=VMEM_LIMIT),
        name="rglru_mixer",
    )(x, gains, w_in, conv_w, conv_b, gate_w, gate_b, a_param, w_out)


def _ffn_kernel(x_ref, g_ref, win_ref, wout_ref, o_ref, *, d_ff):
    x = x_ref[...]
    hn = _rms(x, g_ref[0:1, :]).astype(BF16)
    f = None
    for n in range(d_ff // FFN_CHUNK):
        gate = _dot(hn, win_ref[:, n * FFN_CHUNK:(n + 1) * FFN_CHUNK])
        up = _dot(hn, win_ref[:, d_ff + n * FFN_CHUNK:d_ff + (n + 1) * FFN_CHUNK])
        act = (gate * jax.nn.sigmoid(gate) * up).astype(BF16)
        part = _dot(act, wout_ref[n * FFN_CHUNK:(n + 1) * FFN_CHUNK, :])
        f = part if f is None else f + part
    o_ref[...] = x + _rms(f, g_ref[1:2, :])


def _ffn_layer(x, gains, w_in, w_out, layer):
    b, s, d = x.shape
    d_ff = w_out.shape[1]
    tm = FFN_ROWS
    t = b * s
    assert t % tm == 0 and d_ff % FFN_CHUNK == 0
    kern = functools.partial(_ffn_kernel, d_ff=d_ff)
    out = pl.pallas_call(
        kern,
        grid=(t // tm,),
        in_specs=[
            pl.BlockSpec((tm, d), lambda i: (i, 0)),
            _const_spec(gains.shape),
            _layer_spec(w_in, layer),
            _layer_spec(w_out, layer),
        ],
        out_specs=pl.BlockSpec((tm, d), lambda i: (i, 0)),
        out_shape=jax.ShapeDtypeStruct((t, d), x.dtype),
        compiler_params=pltpu.CompilerParams(
            dimension_semantics=("arbitrary",),
            vmem_limit_bytes=VMEM_LIMIT),
        name="swiglu_ffn",
    )(x.reshape(t, d), gains, w_in, w_out)
    return out.reshape(b, s, d)


def kernel(x, ret_w_in, ret_w_out, lru_w_in, lru_conv_w, lru_conv_b, lru_gate_w, lru_gate_b, lru_a_param, lru_w_out, norm_g, ffn_w_in, ffn_w_out):
    depth = norm_g.shape[0]
    ret_w_in, ret_w_out = ret_w_in.astype(BF16), ret_w_out.astype(BF16)
    lru_w_in, lru_gate_w, lru_w_out = lru_w_in.astype(BF16), lru_gate_w.astype(BF16), lru_w_out.astype(BF16)
    ffn_w_in, ffn_w_out = ffn_w_in.astype(BF16), ffn_w_out.astype(BF16)
    lru_conv_b = lru_conv_b[:, None, :]
    lru_gate_b = lru_gate_b.reshape(lru_gate_b.shape[0], 2, -1)
    lru_a_param = lru_a_param[:, None, :]
    qk_w = ret_w_out.shape[1] // RET_DV * RET_DK
    for layer in range(depth):
        j = layer // 2
        if layer % 2 == 0:
            w_kt = ret_w_in[j, :, qk_w:2 * qk_w].T
            x = _retention_layer(x, norm_g[layer, 0:2], ret_w_in, w_kt, ret_w_out, j)
        else:
            x = _lru_layer(x, norm_g[layer, 0:2], lru_w_in, lru_conv_w, lru_conv_b, lru_gate_w, lru_gate_b,
                           lru_a_param, lru_w_out, j)
        x = _ffn_layer(x, norm_g[layer, 2:4], ffn_w_in, ffn_w_out, layer)
    return x
```

```python
import functools
import math

import jax
import jax.numpy as jnp
import numpy as np
from jax import lax
from jax.experimental import pallas as pl
from jax.experimental.pallas import tpu as pltpu

F32 = jnp.float32
BF16 = jnp.bfloat16

NORM_EPS = 1e-6
ROPE_BASE = 10000.0
F32_TINY = float(jnp.finfo(jnp.float32).tiny)

RET_DK = 256
RET_DV = 512
RET_CHUNK = 256
LRU_BLOCK = 256
LRU_CONV = 4
LRU_C = 8.0
SUBLANES = 8
LANES = 128
FFN_CHUNK = 256

MIX_ROWS = 256
LRU_STEP_ROWS = 512
FFN_ROWS = 512
VMEM_LIMIT = 56 * 1024 * 1024


def _rms(x, g):
    ms = jnp.mean(x * x, axis=-1, keepdims=True)
    return x * lax.rsqrt(ms + NORM_EPS) * g


def _dot(a, b):
    return jnp.dot(a, b, preferred_element_type=F32)


def _dot_nt(a, b):
    return lax.dot_general(a, b, (((1,), (1,)), ((), ())), preferred_element_type=F32)


def _const_spec(shape):
    zeros = (0,) * len(shape)
    return pl.BlockSpec(shape, lambda *_: zeros, pipeline_mode=pl.Buffered(1))


def _layer_spec(stacked, layer):
    index = (layer,) + (0,) * (stacked.ndim - 1)
    return pl.BlockSpec((None,) + stacked.shape[1:], lambda *_: index, pipeline_mode=pl.Buffered(1))


def _ret_kernel(x_ref, g_ref, rs_ref, rst_ref, rw_ref, rwt_ref, win_ref, wout_ref,
                o_ref, state_ref, decay_ref, qdec_ref, kdec_ref, og_ref, wkt_ref, *, heads):
    c = RET_CHUNK
    half = RET_DK // 2
    qk_w = heads * RET_DK
    v_w = heads * RET_DV
    log_gamma = [math.log1p(-2.0 ** (-5.0 - h)) for h in range(heads)]

    @pl.when((pl.program_id(0) == 0) & (pl.program_id(1) == 0))
    def _init_tables():
        for i in range(win_ref.shape[0] // RET_DK):
            for j in range(heads):
                blk = win_ref[i * RET_DK:(i + 1) * RET_DK, qk_w + j * RET_DK:qk_w + (j + 1) * RET_DK]
                wkt_ref[j * RET_DK:(j + 1) * RET_DK, i * RET_DK:(i + 1) * RET_DK] = blk.T
        row = lax.broadcasted_iota(jnp.int32, (c, c), 0)
        col = lax.broadcasted_iota(jnp.int32, (c, c), 1)
        rel = (row - col).astype(F32)
        causal = row >= col
        rq = lax.broadcasted_iota(jnp.int32, (c, RET_DK), 0).astype(F32)
        ck = lax.broadcasted_iota(jnp.int32, (RET_DK, c), 1).astype(F32)
        for h in range(heads):
            lg = log_gamma[h]
            scale = RET_DK ** -0.5
            decay_ref[h] = jnp.where(causal, jnp.exp(jnp.where(causal, rel, 0.0) * lg), 0.0) * scale
            qdec_ref[h] = jnp.exp((rq + 1.0) * lg)
            kdec_ref[h] = jnp.exp((c - 1.0 - ck) * lg) * scale

    @pl.when(pl.program_id(1) == 0)
    def _reset_state():
        state_ref[...] = jnp.zeros_like(state_ref)

    x = x_ref[0]
    hn = _rms(x, g_ref[0:1, :]).astype(BF16)
    ca, sa = rs_ref[0, 0:1, :], rs_ref[0, 1:2, :]
    cb, sb = rw_ref[0], rw_ref[1]
    cos = ca * cb - sa * sb
    sin = sa * cb + ca * sb
    cat = jnp.concatenate([rst_ref[0, 0]] * (c // rst_ref.shape[-1]), axis=1)
    sat = jnp.concatenate([rst_ref[0, 1]] * (c // rst_ref.shape[-1]), axis=1)
    cbt, sbt = rwt_ref[0], rwt_ref[1]
    cos_t = cat * cbt - sat * sbt
    sin_t = sat * cbt + cat * sbt

    k_all = _dot_nt(wkt_ref[...], hn)

    for h in range(heads):
        q = _dot(hn, win_ref[:, h * RET_DK:(h + 1) * RET_DK])
        v = _dot(hn, win_ref[:, 2 * qk_w + h * RET_DV:2 * qk_w + (h + 1) * RET_DV])
        gate = _dot(hn, win_ref[:, 2 * qk_w + v_w + h * RET_DV:2 * qk_w + v_w + (h + 1) * RET_DV])
        q1, q2 = q[:, :half], q[:, half:]
        q = jnp.concatenate([q1 * cos - q2 * sin, q1 * sin + q2 * cos], axis=-1)
        k1 = k_all[h * RET_DK:h * RET_DK + half, :]
        k2 = k_all[h * RET_DK + half:(h + 1) * RET_DK, :]
        k = jnp.concatenate([k1 * cos_t - k2 * sin_t, k1 * sin_t + k2 * cos_t], axis=0)
        qb = q.astype(BF16)
        kb = k.astype(BF16)
        vb = v.astype(BF16)
        qd = (q * qdec_ref[h]).astype(BF16)
        kd = (k * kdec_ref[h]).astype(BF16)

        state = state_ref[h]
        scores = (_dot(qb, kb) * decay_ref[h]).astype(BF16)
        o = _dot(scores, vb) + _dot(qd, state.astype(BF16))
        state_ref[h] = state * math.exp(c * log_gamma[h]) + _dot(kd, vb)

        mu = jnp.mean(o, axis=-1, keepdims=True)
        oc = o - mu
        on = oc * lax.rsqrt(jnp.mean(oc * oc, axis=-1, keepdims=True) + NORM_EPS)
        og_ref[:, h * RET_DV:(h + 1) * RET_DV] = (on * (gate * jax.nn.sigmoid(gate))).astype(BF16)

    m = _dot(og_ref[...], wout_ref[...])
    o_ref[0] = x + _rms(m, g_ref[1:2, :])


def _rope_tables(seq, ts):
    half = RET_DK // 2
    inv = 1.0 / (ROPE_BASE ** (np.arange(half, dtype=np.float64) / half))
    a_start = (np.arange(seq // ts, dtype=np.float64) * ts)[:, None] * inv[None, :]
    a_within = np.arange(ts, dtype=np.float64)[:, None] * inv[None, :]
    start = np.zeros((seq // ts, SUBLANES, half), np.float32)
    start[:, 0], start[:, 1] = np.cos(a_start), np.sin(a_start)
    start_t = np.stack([np.cos(a_start), np.sin(a_start)], axis=1).astype(np.float32)
    within = np.stack([np.cos(a_within), np.sin(a_within)]).astype(np.float32)
    start_t = jnp.broadcast_to(jnp.asarray(start_t)[..., None], start_t.shape + (LANES,))
    return jnp.asarray(start), start_t, jnp.asarray(within), jnp.asarray(within.transpose(0, 2, 1))


def _retention_layer(x, gains, w_in, w_out, layer):
    b, s, d = x.shape
    heads = w_out.shape[1] // RET_DV
    ts = MIX_ROWS
    half = RET_DK // 2
    assert ts == RET_CHUNK and s % ts == 0 and ts % LANES == 0
    rope_start, rope_start_t, rope_within, rope_within_t = _rope_tables(s, ts)
    kern = functools.partial(_ret_kernel, heads=heads)
    return pl.pallas_call(
        kern,
        grid=(b, s // ts),
        in_specs=[
            pl.BlockSpec((1, ts, d), lambda i, j: (i, j, 0)),
            _const_spec(gains.shape),
            pl.BlockSpec((1, SUBLANES, half), lambda i, j: (j, 0, 0)),
            pl.BlockSpec((1, 2, half, LANES), lambda i, j: (j, 0, 0, 0)),
            _const_spec(rope_within.shape),
            _const_spec(rope_within_t.shape),
            _layer_spec(w_in, layer),
            _layer_spec(w_out, layer),
        ],
        out_specs=pl.BlockSpec((1, ts, d), lambda i, j: (i, j, 0)),
        out_shape=jax.ShapeDtypeStruct(x.shape, x.dtype),
        scratch_shapes=[
            pltpu.VMEM((heads, RET_DK, RET_DV), F32),
            pltpu.VMEM((heads, RET_CHUNK, RET_CHUNK), F32),
            pltpu.VMEM((heads, RET_CHUNK, RET_DK), F32),
            pltpu.VMEM((heads, RET_DK, RET_CHUNK), F32),
            pltpu.VMEM((ts, heads * RET_DV), BF16),
            pltpu.VMEM((heads * RET_DK, d), BF16),
        ],
        compiler_params=pltpu.CompilerParams(
            dimension_semantics=("arbitrary", "arbitrary"),
            vmem_limit_bytes=VMEM_LIMIT),
        name="retention_mixer",
    )(x, gains, rope_start, rope_start_t, rope_within, rope_within_t, w_in, w_out)


def _sublane_scan(a, b, sub):
    d = 1
    while d < SUBLANES:
        keep = sub >= d
        a_prev = pltpu.roll(a, d, axis=0)
        b_prev = pltpu.roll(b, d, axis=0)
        b = jnp.where(keep, a * b_prev + b, b)
        a = jnp.where(keep, a * a_prev, a)
        d *= 2
    return a, b


def _lru_kernel(x_ref, g_ref, win_ref, convw_ref, convb_ref, gw_ref, gb_ref, ap_ref, wout_ref,
                o_ref, perm_ref, permt_ref, hp_ref, ubuf_ref, tail_ref, carry_ref, hy_ref, hyn_ref, m_ref,
                *, width, rows):
    nsub = x_ref.shape[1] // rows
    seg = rows // SUBLANES
    nblk = width // LRU_BLOCK
    halo = (LRU_CONV - 1) * SUBLANES
    d_model = x_ref.shape[2]
    ncb = d_model // LRU_BLOCK

    @pl.when((pl.program_id(0) == 0) & (pl.program_id(1) == 0))
    def _init_perm():
        row = lax.broadcasted_iota(jnp.int32, (rows, rows), 0)
        col = lax.broadcasted_iota(jnp.int32, (rows, rows), 1)
        perm_ref[...] = jnp.where(col == (row % SUBLANES) * seg + row // SUBLANES, 1.0, 0.0).astype(BF16)
        permt_ref[...] = jnp.where(row == (col % SUBLANES) * seg + col // SUBLANES, 1.0, 0.0).astype(BF16)

    @pl.when(pl.program_id(1) == 0)
    def _reset():
        tail_ref[...] = jnp.zeros_like(tail_ref)
        carry_ref[...] = jnp.zeros_like(carry_ref)

    sub = lax.broadcasted_iota(jnp.int32, (SUBLANES, LRU_BLOCK), 0)

    def x_rows(k):
        return x_ref[0, k * rows:(k + 1) * rows, :]

    def a_head(k):
        hn = _rms(x_rows(k), g_ref[0:1, :]).astype(BF16)
        hp_ref[k] = _dot(perm_ref[...], hn).astype(BF16)

    def a_slice(k, n):
        cols = slice(n * LRU_BLOCK, (n + 1) * LRU_BLOCK)
        u = _dot(hp_ref[k], win_ref[:, width + n * LRU_BLOCK:width + (n + 1) * LRU_BLOCK])
        ubuf_ref[k, halo:halo + rows, cols] = u
        for j in range(1, LRU_CONV):
            cur = u[rows - j * SUBLANES:rows - (j - 1) * SUBLANES, :]
            prev = tail_ref[halo - j * SUBLANES:halo - (j - 1) * SUBLANES, cols]
            ubuf_ref[k, halo - j * SUBLANES:halo - (j - 1) * SUBLANES, cols] = jnp.where(
                sub == 0, pltpu.roll(prev, 1, axis=0), pltpu.roll(cur, 1, axis=0))
        tail_ref[:, cols] = u[rows - halo:rows, :]

    def b_block(k, n):
        cols = slice(n * LRU_BLOCK, (n + 1) * LRU_BLOCK)
        u = convb_ref[:, cols]
        for j in range(LRU_CONV):
            u = u + convw_ref[j:j + 1, cols] * ubuf_ref[k, j * SUBLANES:j * SUBLANES + rows, cols]
        ub = u.astype(BF16)
        r = jax.nn.sigmoid(_dot(ub, gw_ref[0, n]) + gb_ref[0:1, cols])
        i = jax.nn.sigmoid(_dot(ub, gw_ref[1, n]) + gb_ref[1:2, cols])
        log_a = (-LRU_C) * r * jax.nn.softplus(-ap_ref[:, cols])
        a = jnp.exp(log_a)
        z = -jnp.tanh(log_a) * (1.0 + a * a)
        b = z * lax.rsqrt(jnp.maximum(z, F32_TINY)) * (i * u)

        hs, ps = [], []
        h = p = None
        for l in range(seg):
            al = a[l * SUBLANES:(l + 1) * SUBLANES, :]
            bl = b[l * SUBLANES:(l + 1) * SUBLANES, :]
            h = bl if h is None else al * h + bl
            p = al if p is None else al * p
            hs.append(h)
            ps.append(p)
        carry = carry_ref[:, cols]
        pa, pb = _sublane_scan(p, h, sub)
        h_end = pa * carry + pb
        h_in = jnp.where(sub == 0, carry, pltpu.roll(h_end, 1, axis=0))
        carry_ref[:, cols] = jnp.broadcast_to(h_end[SUBLANES - 1:SUBLANES, :], (SUBLANES, LRU_BLOCK))

        y = jax.nn.gelu(_dot(hp_ref[k], win_ref[:, cols]), approximate=True)
        hy = [(hs[l] + ps[l] * h_in) * y[l * SUBLANES:(l + 1) * SUBLANES, :] for l in range(seg)]
        hy_ref[k, :, cols] = jnp.concatenate(hy, axis=0).astype(BF16)

    def c_unperm(k, n):
        cols = slice(n * LRU_BLOCK, (n + 1) * LRU_BLOCK)
        hyn_ref[k, :, cols] = _dot(permt_ref[...], hy_ref[k, :, cols]).astype(BF16)

    def c_wout(k, cb):
        cols = slice(cb * LRU_BLOCK, (cb + 1) * LRU_BLOCK)
        m_ref[k, :, cols] = _dot(hyn_ref[k], wout_ref[:, cols])

    def c_tail(k):
        o_ref[0, k * rows:(k + 1) * rows, :] = x_rows(k) + _rms(m_ref[k], g_ref[1:2, :])

    def c_slot(k, n):
        if n == 0:
            for j in range(nblk):
                c_unperm(k, j)
        elif n <= ncb:
            c_wout(k, n - 1)
        if n == min(ncb, nblk - 1):
            c_tail(k)

    a_head(0)
    for n in range(nblk):
        a_slice(0, n)
    for k in range(nsub):
        for n in range(nblk):
            b_block(k, n)
            if k + 1 < nsub:
                if n == 0:
                    a_head(k + 1)
                a_slice(k + 1, n)
            if k > 0:
                c_slot(k - 1, n)
    for n in range(nblk):
        c_slot(nsub - 1, n)


def _lru_layer(x, gains, w_in, conv_w, conv_b, gate_w, gate_b, a_param, w_out, layer):
    b, s, d = x.shape
    width = w_out.shape[1]
    ts = LRU_STEP_ROWS
    rows = MIX_ROWS
    nsub = ts // rows
    halo = (LRU_CONV - 1) * SUBLANES
    assert s % ts == 0 and ts % rows == 0 and width % LRU_BLOCK == 0 and d % LRU_BLOCK == 0
    assert rows // SUBLANES >= LRU_CONV and d // LRU_BLOCK < width // LRU_BLOCK
    kern = functools.partial(_lru_kernel, width=width, rows=rows)
    return pl.pallas_call(
        kern,
        grid=(b, s // ts),
        in_specs=[
            pl.BlockSpec((1, ts, d), lambda i, j: (i, j, 0)),
            _const_spec(gains.shape),
            _layer_spec(w_in, layer),
            _layer_spec(conv_w, layer),
            _layer_spec(conv_b, layer),
            _layer_spec(gate_w, layer),
            _layer_spec(gate_b, layer),
            _layer_spec(a_param, layer),
            _layer_spec(w_out, layer),
        ],
        out_specs=pl.BlockSpec((1, ts, d), lambda i, j: (i, j, 0)),
        out_shape=jax.ShapeDtypeStruct(x.shape, x.dtype),
        scratch_shapes=[
            pltpu.VMEM((rows, rows), BF16),
            pltpu.VMEM((rows, rows), BF16),
            pltpu.VMEM((nsub, rows, d), BF16),
            pltpu.VMEM((nsub, halo + rows, width), F32),
            pltpu.VMEM((halo, width), F32),
            pltpu.VMEM((SUBLANES, width), F32),
            pltpu.VMEM((nsub, rows, width), BF16),
            pltpu.VMEM((nsub, rows, width), BF16),
            pltpu.VMEM((nsub, rows, d), F32),
        ],
        compiler_params=pltpu.CompilerParams(
            dimension_semantics=("arbitrary", "arbitrary"),
            vmem_limit_bytes=VMEM_LIMIT),
        name="rglru_mixer",
    )(x, gains, w_in, conv_w, conv_b, gate_w, gate_b, a_param, w_out)


def _ffn_kernel(x_ref, g_ref, win_ref, wout_ref, o_ref, *, d_ff):
    x = x_ref[...]
    hn = _rms(x, g_ref[0:1, :]).astype(BF16)
    f = None
    for n in range(d_ff // FFN_CHUNK):
        gate = _dot(hn, win_ref[:, n * FFN_CHUNK:(n + 1) * FFN_CHUNK])
        up = _dot(hn, win_ref[:, d_ff + n * FFN_CHUNK:d_ff + (n + 1) * FFN_CHUNK])
        act = (gate * jax.nn.sigmoid(gate) * up).astype(BF16)
        part = _dot(act, wout_ref[n * FFN_CHUNK:(n + 1) * FFN_CHUNK, :])
        f = part if f is None else f + part
    o_ref[...] = x + _rms(f, g_ref[1:2, :])


def _ffn_layer(x, gains, w_in, w_out, layer):
    b, s, d = x.shape
    d_ff = w_out.shape[1]
    tm = FFN_ROWS
    t = b * s
    assert t % tm == 0 and d_ff % FFN_CHUNK == 0
    kern = functools.partial(_ffn_kernel, d_ff=d_ff)
    out = pl.pallas_call(
        kern,
        grid=(t // tm,),
        in_specs=[
            pl.BlockSpec((tm, d), lambda i: (i, 0)),
            _const_spec(gains.shape),
            _layer_spec(w_in, layer),
            _layer_spec(w_out, layer),
        ],
        out_specs=pl.BlockSpec((tm, d), lambda i: (i, 0)),
        out_shape=jax.ShapeDtypeStruct((t, d), x.dtype),
        compiler_params=pltpu.CompilerParams(
            dimension_semantics=("arbitrary",),
            vmem_limit_bytes=VMEM_LIMIT),
        name="swiglu_ffn",
    )(x.reshape(t, d), gains, w_in, w_out)
    return out.reshape(b, s, d)


def kernel(x, ret_w_in, ret_w_out, lru_w_in, lru_conv_w, lru_conv_b, lru_gate_w, lru_gate_b, lru_a_param, lru_w_out, norm_g, ffn_w_in, ffn_w_out):
    depth = norm_g.shape[0]
    ret_w_in, ret_w_out = ret_w_in.astype(BF16), ret_w_out.astype(BF16)
    lru_w_in, lru_gate_w, lru_w_out = lru_w_in.astype(BF16), lru_gate_w.astype(BF16), lru_w_out.astype(BF16)
    ffn_w_in, ffn_w_out = ffn_w_in.astype(BF16), ffn_w_out.astype(BF16)
    lru_conv_b = lru_conv_b[:, None, :]
    lru_gate_b = lru_gate_b.reshape(lru_gate_b.shape[0], 2, -1)
    lru_a_param = lru_a_param[:, None, :]
    for layer in range(depth):
        j = layer // 2
        if layer % 2 == 0:
            x = _retention_layer(x, norm_g[layer, 0:2], ret_w_in, ret_w_out, j)
        else:
            x = _lru_layer(x, norm_g[layer, 0:2], lru_w_in, lru_conv_w, lru_conv_b, lru_gate_w, lru_gate_b,
                           lru_a_param, lru_w_out, j)
        x = _ffn_layer(x, norm_g[layer, 2:4], ffn_w_in, ffn_w_out, layer)
    return x
```
